```python
import jax, jax.numpy as jnp
from jax import lax
import numpy as np

D_MODEL = 2048
BATCH = 16
SEQ = 2048
DEPTH = 2

N_MEM = 256
HEAD_DIM = D_MODEL // 16
N_MIX_HEADS = 12
N_MEM_HEADS = 4
MIX_W = N_MIX_HEADS * HEAD_DIM
MEM_W = N_MEM_HEADS * HEAD_DIM
D_MIX = MIX_W + MEM_W
D_FF = 256 * ((8 * D_MODEL // 3 + 255) // 256)
CHUNK = 128
Q_BLOCK = 128
N_A = DEPTH // 2
N_B = DEPTH - N_A
ROPE_BASE = 10000.0
EPS = 1e-6
MACARON_W = 0.5

kernel_name = "yoco_retention_stickbreaking_macaron_memory"


def rmsnorm(x, g):
    xf = x.astype(jnp.float32)
    y = xf * lax.rsqrt(jnp.mean(xf * xf, axis=-1, keepdims=True) + EPS)
    return (y * g.astype(jnp.float32)).astype(x.dtype)


def head_norm(y):
    yf = y.astype(jnp.float32)
    mu = jnp.mean(yf, axis=-1, keepdims=True)
    var = jnp.mean(jnp.square(yf - mu), axis=-1, keepdims=True)
    return ((yf - mu) * lax.rsqrt(var + EPS)).astype(y.dtype)


def swiglu(h, w_gate, w_up, w_down):
    return (jax.nn.silu(h @ w_gate) * (h @ w_up)) @ w_down


def rotary(t, positions):
    d = t.shape[-1]
    inv = ROPE_BASE ** (-jnp.arange(0, d, 2, dtype=jnp.float32) / d)
    ang = positions.astype(jnp.float32)[..., None] * inv
    cos = jnp.cos(ang)[:, :, None, :]
    sin = jnp.sin(ang)[:, :, None, :]
    tf = t.astype(jnp.float32)
    t1, t2 = tf[..., : d // 2], tf[..., d // 2:]
    return jnp.concatenate([t1 * cos - t2 * sin, t2 * cos + t1 * sin], axis=-1).astype(t.dtype)


def retention_log_decay():
    return jnp.log1p(-jnp.exp2(-5.0 - jnp.arange(N_MIX_HEADS, dtype=jnp.float32)))


def retention_chunkwise(q, k, v):
    B, S, H, d = q.shape
    N = S // CHUNK
    dt = q.dtype
    to_chunks = lambda t: t.reshape(B, N, CHUNK, H, d).transpose(0, 3, 1, 2, 4)
    qc, kc, vc = to_chunks(q), to_chunks(k), to_chunks(v)
    lg = retention_log_decay()
    idx = jnp.arange(CHUNK, dtype=jnp.float32)
    diff = idx[:, None] - idx[None, :]
    dmask = jnp.where(diff >= 0, jnp.exp(lg[:, None, None] * jnp.maximum(diff, 0.0)), 0.0)
    scores = jnp.einsum('bhncd,bhnmd->bhncm', qc, kc) * dmask[None, :, None].astype(dt)
    inner = jnp.einsum('bhncm,bhnmd->bhncd', scores, vc)
    k_decay = jnp.exp(lg[:, None] * (CHUNK - 1 - idx)[None, :]).astype(dt)
    kv = jnp.einsum('bhnmd,bhnme->bhnde', kc * k_decay[None, :, None, :, None], vc)
    chunk_decay = jnp.exp(lg * CHUNK).astype(dt)

    def step(state, kv_i):
        return state * chunk_decay[None, :, None, None] + kv_i, state

    _, prev = lax.scan(step, jnp.zeros((B, H, d, d), dt), jnp.moveaxis(kv, 2, 0))
    prev = jnp.moveaxis(prev, 0, 2)
    q_decay = jnp.exp(lg[:, None] * (idx + 1.0)[None, :]).astype(dt)
    cross = jnp.einsum('bhncd,bhnde->bhnce', qc * q_decay[None, :, None, :, None], prev)
    out = inner + cross
    return out.transpose(0, 2, 3, 1, 4).reshape(B, S, H, d)


def stick_breaking(q, k, v):
    B, S, H, d = q.shape
    scale = d ** -0.5
    outs = []
    for i in range(S // Q_BLOCK):
        L = (i + 1) * Q_BLOCK
        qs = q[:, i * Q_BLOCK:L]
        z = jnp.einsum('bqhd,bkhd->bhqk', qs, k[:, :L]).astype(jnp.float32) * scale
        t_pos = i * Q_BLOCK + jnp.arange(Q_BLOCK)
        causal = jnp.arange(L)[None, :] < t_pos[:, None]
        log_beta = jax.nn.log_sigmoid(z)
        log_1mb = jnp.where(causal, jax.nn.log_sigmoid(-z), 0.0)
        after = lax.cumsum(log_1mb, axis=3, reverse=True) - log_1mb
        A = jnp.where(causal, jnp.exp(log_beta + after), 0.0)
        outs.append(jnp.einsum('bhqk,bkhd->bqhd', A.astype(v.dtype), v[:, :L]))
    return jnp.concatenate(outs, axis=1)


def memory_attention(qm, mem_n, w_mem_kv):
    B, S, _ = qm.shape
    M = mem_n.shape[1]
    mkv = mem_n @ w_mem_kv
    mk = mkv[..., :MEM_W].reshape(B, M, N_MEM_HEADS, HEAD_DIM)
    mv = mkv[..., MEM_W:].reshape(B, M, N_MEM_HEADS, HEAD_DIM)
    qh = qm.reshape(B, S, N_MEM_HEADS, HEAD_DIM)
    s = jnp.einsum('bshd,bmhd->bhsm', qh, mk).astype(jnp.float32) * (HEAD_DIM ** -0.5)
    p = jax.nn.softmax(s, axis=-1).astype(mv.dtype)
    return jnp.einsum('bhsm,bmhd->bshd', p, mv).reshape(B, S, MEM_W)


def mixer_a(h, mem_n, positions, w_in, w_mem_kv, w_o):
    B, S, _ = h.shape
    proj = h @ w_in
    q = proj[..., 0:MIX_W].reshape(B, S, N_MIX_HEADS, HEAD_DIM)
    k = proj[..., MIX_W:2 * MIX_W].reshape(B, S, N_MIX_HEADS, HEAD_DIM)
    v = proj[..., 2 * MIX_W:3 * MIX_W].reshape(B, S, N_MIX_HEADS, HEAD_DIM)
    g = proj[..., 3 * MIX_W:4 * MIX_W]
    qm = proj[..., 4 * MIX_W:]
    q = rotary(q, positions)
    k = rotary(k, positions) * (HEAD_DIM ** -0.5)
    y = retention_chunkwise(q, k, v)
    y = head_norm(y).reshape(B, S, MIX_W) * jax.nn.silu(g)
    ym = memory_attention(qm, mem_n, w_mem_kv)
    return jnp.concatenate([y, ym], axis=-1) @ w_o


def mixer_b(h, mem_n, k_sh, v_sh, w_in, w_mem_kv, w_o):
    B, S, _ = h.shape
    proj = h @ w_in
    q = proj[..., :MIX_W].reshape(B, S, N_MIX_HEADS, HEAD_DIM)
    qm = proj[..., MIX_W:]
    y = stick_breaking(q, k_sh, v_sh).reshape(B, S, MIX_W)
    ym = memory_attention(qm, mem_n, w_mem_kv)
    return jnp.concatenate([y, ym], axis=-1) @ w_o


def setup_inputs(seed: int = 0) -> dict:
    key = jax.random.key(seed)
    ks = iter(jax.random.split(key, 32))
    f32 = jnp.float32

    def w(shape, fan_in):
        return jax.random.normal(next(ks), shape, f32) * (fan_in ** -0.5)

    def gain(shape):
        return 1.0 + 0.02 * jax.random.normal(next(ks), shape, f32)

    x = jax.random.normal(next(ks), (BATCH, SEQ, D_MODEL), f32)
    mem = jax.random.normal(next(ks), (BATCH, N_MEM, D_MODEL), f32)
    offset = jax.random.randint(next(ks), (BATCH, 1), 0, 4096, dtype=jnp.int32)
    positions = (jnp.arange(SEQ, dtype=jnp.int32)[None, :] + offset).astype(jnp.int32)
    return {
        "x": x,
        "mem": mem,
        "positions": positions,
        "ffn1_norm_pre": gain((DEPTH, D_MODEL)),
        "ffn1_norm_post": gain((DEPTH, D_MODEL)),
        "ffn1_w_gate": w((DEPTH, D_MODEL, D_FF), D_MODEL),
        "ffn1_w_up": w((DEPTH, D_MODEL, D_FF), D_MODEL),
        "ffn1_w_down": w((DEPTH, D_FF, D_MODEL), D_FF),
        "mix_norm_pre": gain((DEPTH, D_MODEL)),
        "mix_norm_post": gain((DEPTH, D_MODEL)),
        "mem_norm": gain((DEPTH, D_MODEL)),
        "w_mem_kv": w((DEPTH, D_MODEL, 2 * MEM_W), D_MODEL),
        "w_o": w((DEPTH, D_MIX, D_MODEL), D_MIX),
        "ret_w_in": w((N_A, D_MODEL, 4 * MIX_W + MEM_W), D_MODEL),
        "kv_norm": gain((D_MODEL,)),
        "w_kv_shared": w((D_MODEL, 2 * MIX_W), D_MODEL),
        "sb_w_in": w((N_B, D_MODEL, MIX_W + MEM_W), D_MODEL),
        "ffn2_norm_pre": gain((DEPTH, D_MODEL)),
        "ffn2_norm_post": gain((DEPTH, D_MODEL)),
        "ffn2_w_gate": w((DEPTH, D_MODEL, D_FF), D_MODEL),
        "ffn2_w_up": w((DEPTH, D_MODEL, D_FF), D_MODEL),
        "ffn2_w_down": w((DEPTH, D_FF, D_MODEL), D_FF),
    }


def reference(x, mem, positions, ffn1_norm_pre, ffn1_norm_post, ffn1_w_gate, ffn1_w_up, ffn1_w_down,
              mix_norm_pre, mix_norm_post, mem_norm, w_mem_kv, w_o, ret_w_in, kv_norm, w_kv_shared,
              sb_w_in, ffn2_norm_pre, ffn2_norm_post, ffn2_w_gate, ffn2_w_up, ffn2_w_down):
    B, S, _ = x.shape
    k_sh = v_sh = None
    for l in range(DEPTH):
        if l == N_A:
            hs = rmsnorm(x, kv_norm)
            kv = hs @ w_kv_shared
            k_sh = kv[..., :MIX_W].reshape(B, S, N_MIX_HEADS, HEAD_DIM)
            v_sh = kv[..., MIX_W:].reshape(B, S, N_MIX_HEADS, HEAD_DIM)
        h = rmsnorm(x, ffn1_norm_pre[l])
        x = x + MACARON_W * rmsnorm(swiglu(h, ffn1_w_gate[l], ffn1_w_up[l], ffn1_w_down[l]), ffn1_norm_post[l])
        mem_n = rmsnorm(mem, mem_norm[l])
        h = rmsnorm(x, mix_norm_pre[l])
        if l < N_A:
            y = mixer_a(h, mem_n, positions, ret_w_in[l], w_mem_kv[l], w_o[l])
        else:
            y = mixer_b(h, mem_n, k_sh, v_sh, sb_w_in[l - N_A], w_mem_kv[l], w_o[l])
        x = x + rmsnorm(y, mix_norm_post[l])
        h = rmsnorm(x, ffn2_norm_pre[l])
        x = x + MACARON_W * rmsnorm(swiglu(h, ffn2_w_gate[l], ffn2_w_up[l], ffn2_w_down[l]), ffn2_norm_post[l])
    return x
```

```python
import functools

import jax
import jax.numpy as jnp
from jax import lax
from jax.experimental import pallas as pl
from jax.experimental.pallas import tpu as pltpu

HEAD_DIM = 128
N_MIX_HEADS = 12
N_MEM_HEADS = 4
MIX_W = N_MIX_HEADS * HEAD_DIM
MEM_W = N_MEM_HEADS * HEAD_DIM
CHUNK = 128
ROPE_BASE = 10000.0
EPS = 1e-6
MACARON_W = 0.5

HEADS_PER_TILE = 4
PROJ_TILE = HEADS_PER_TILE * HEAD_DIM
V7X_SCOPED_VMEM_CAP = 60000 * 1024

F32 = jnp.float32
BF16 = jnp.bfloat16


def _vmem_limit(estimate_bytes):
    return int(min(2 * estimate_bytes, V7X_SCOPED_VMEM_CAP))


def _rms(x, gain):
    ms = jnp.mean(x * x, axis=-1, keepdims=True)
    return x * lax.rsqrt(ms + EPS) * gain


def _dot(a, b):
    return jnp.dot(a, b, preferred_element_type=F32)


def _dot_nt(a, b):
    return lax.dot_general(a, b, (((1,), (1,)), ((), ())), preferred_element_type=F32)


def _dot_tn(a, b):
    return lax.dot_general(a, b, (((0,), (0,)), ((), ())), preferred_element_type=F32)


def _ffn_kernel(x_ref, gpre_ref, gpost_ref, wgu_ref, wd_ref, o_ref, h_ref, *, bf):
    j = pl.program_id(1)

    @pl.when(j == 0)
    def _():
        h_ref[...] = _rms(x_ref[...], gpre_ref[...]).astype(BF16)
        o_ref[...] = jnp.zeros_like(o_ref)

    r = _dot(h_ref[...], wgu_ref[...])
    a = (jax.nn.silu(r[:, :bf]) * r[:, bf:]).astype(BF16)
    o_ref[...] += _dot(a, wd_ref[...])

    @pl.when(j == pl.num_programs(1) - 1)
    def _():
        o_ref[...] = x_ref[...] + MACARON_W * _rms(o_ref[...], gpost_ref[...])


def _interleave_gate_up(w_gate, w_up, bf):
    d, f = w_gate.shape
    g = w_gate.astype(BF16).reshape(d, f // bf, bf)
    u = w_up.astype(BF16).reshape(d, f // bf, bf)
    return jnp.concatenate([g, u], axis=2).reshape(d, 2 * f)


def _ffn(x2d, g_pre, g_post, w_gate, w_up, w_down, *, bm, bf):
    m, d = x2d.shape
    f = w_gate.shape[1]
    wgu = _interleave_gate_up(w_gate, w_up, bf)
    wd = w_down.astype(BF16)
    est = (2 * 2 * bm * d * 4
           + bm * d * 2
           + 2 * (d * 2 * bf + bf * d) * 2
           + bm * 2 * bf * 4 + bm * d * 4)
    return pl.pallas_call(
        functools.partial(_ffn_kernel, bf=bf),
        grid=(m // bm, f // bf),
        in_specs=[
            pl.BlockSpec((bm, d), lambda i, j: (i, 0)),
            pl.BlockSpec((1, d), lambda i, j: (0, 0)),
            pl.BlockSpec((1, d), lambda i, j: (0, 0)),
            pl.BlockSpec((d, 2 * bf), lambda i, j: (0, j)),
            pl.BlockSpec((bf, d), lambda i, j: (j, 0)),
        ],
        out_specs=pl.BlockSpec((bm, d), lambda i, j: (i, 0)),
        out_shape=jax.ShapeDtypeStruct((m, d), F32),
        scratch_shapes=[pltpu.VMEM((bm, d), BF16)],
        compiler_params=pltpu.CompilerParams(
            dimension_semantics=("parallel", "arbitrary"),
            vmem_limit_bytes=_vmem_limit(est)),
        name="ffn",
    )(x2d, g_pre.reshape(1, d), g_post.reshape(1, d), wgu, wd)


def _proj_kernel(*refs, sections, rotary):
    if rotary:
        x_ref, gain_ref, w_ref, pos_ref, inv_ref = refs[:5]
        rest = refs[5:]
    else:
        x_ref, gain_ref, w_ref = refs[:3]
        rest = refs[3:]
    out_refs = rest[:len(sections)]
    h_ref = rest[len(sections)]
    j = pl.program_id(1)

    @pl.when(j == 0)
    def _():
        h_ref[...] = _rms(x_ref[...], gain_ref[...]).astype(BF16)
        if rotary:
            cos_ref, sin_ref = rest[len(sections) + 1:]
            ang = pos_ref[...] * inv_ref[...]
            lane = lax.broadcasted_iota(jnp.int32, ang.shape, 1)
            sin = jnp.sin(ang)
            cos_ref[...] = jnp.cos(ang)
            sin_ref[...] = jnp.where(lane < HEAD_DIM // 2, -sin, sin)

    r = _dot(h_ref[...], w_ref[...])

    t0 = 0
    for out_ref, (nt, kind, _) in zip(out_refs, sections):
        def write(out_ref=out_ref, kind=kind):
            for hh in range(HEADS_PER_TILE):
                v = r[:, hh * HEAD_DIM:(hh + 1) * HEAD_DIM]
                if kind != "plain":
                    cos_ref, sin_ref = rest[len(sections) + 1:]
                    v = v * cos_ref[...] + pltpu.roll(v, HEAD_DIM // 2, 1) * sin_ref[...]
                    if kind == "rot_scaled":
                        v = v * (HEAD_DIM ** -0.5)
                out_ref[0, hh] = v.astype(out_ref.dtype)

        pl.when((j >= t0) & (j < t0 + nt))(write)
        t0 += nt


def _proj(x2d, gain, w, sections, *, batch, seq, bm, pos=None, inv2=None):
    m, d = x2d.shape
    n = w.shape[1]
    assert n == PROJ_TILE * sum(s[0] for s in sections)
    assert m == batch * seq and seq % bm == 0
    rotary = pos is not None
    spb = seq // bm
    wb = w.astype(BF16)

    in_specs = [
        pl.BlockSpec((bm, d), lambda i, j: (i, 0)),
        pl.BlockSpec((1, d), lambda i, j: (0, 0)),
        pl.BlockSpec((d, PROJ_TILE), lambda i, j: (0, j)),
    ]
    args = [x2d, gain.reshape(1, d), wb]
    scratch = [pltpu.VMEM((bm, d), BF16)]
    if rotary:
        in_specs += [
            pl.BlockSpec((bm, HEAD_DIM), lambda i, j: (i, 0)),
            pl.BlockSpec((1, HEAD_DIM), lambda i, j: (0, 0)),
        ]
        args += [pos, inv2]
        scratch += [pltpu.VMEM((bm, HEAD_DIM), F32), pltpu.VMEM((bm, HEAD_DIM), F32)]

    out_specs, out_shapes = [], []
    t0 = 0
    out_bytes = 0
    for nt, _, dtype in sections:
        def index_map(i, j, t0=t0, nt=nt):
            return (i // spb, jnp.clip(j - t0, 0, nt - 1), i % spb, 0)
        out_specs.append(pl.BlockSpec((1, HEADS_PER_TILE, bm, HEAD_DIM), index_map))
        out_shapes.append(jax.ShapeDtypeStruct((batch, HEADS_PER_TILE * nt, seq, HEAD_DIM), dtype))
        out_bytes += 2 * bm * PROJ_TILE * jnp.dtype(dtype).itemsize
        t0 += nt

    est = (2 * bm * d * 4 + bm * d * 2 + 2 * d * PROJ_TILE * 2 + out_bytes
           + 2 * bm * PROJ_TILE * 4 + 4 * bm * HEAD_DIM * 4)
    return pl.pallas_call(
        functools.partial(_proj_kernel, sections=sections, rotary=rotary),
        grid=(m // bm, n // PROJ_TILE),
        in_specs=in_specs,
        out_specs=out_specs,
        out_shape=out_shapes,
        scratch_shapes=scratch,
        compiler_params=pltpu.CompilerParams(
            dimension_semantics=("parallel", "arbitrary"),
            vmem_limit_bytes=_vmem_limit(est)),
        name="proj",
    )(*args)


def _retention_kernel(lg_ref, q_ref, k_ref, v_ref, g_ref, y_ref, *, n_chunks):
    c = CHUNK
    lg = lg_ref[0]
    row = lax.broadcasted_iota(jnp.int32, (c, c), 0).astype(F32)
    col = lax.broadcasted_iota(jnp.int32, (c, c), 1).astype(F32)
    diff = row - col
    dmask = jnp.where(diff >= 0, jnp.exp(lg * jnp.maximum(diff, 0.0)), 0.0)
    k_decay = jnp.exp(lg * (c - 1.0 - row))
    q_decay = jnp.exp(lg * (row + 1.0))
    chunk_decay = jnp.exp(lg * float(c))

    state = jnp.zeros((HEAD_DIM, HEAD_DIM), F32)
    for n in range(n_chunks):
        sl = pl.ds(n * c, c)
        qc = q_ref[0, 0, sl, :]
        kc = k_ref[0, 0, sl, :]
        vc = v_ref[0, 0, sl, :]
        scores = _dot_nt(qc, kc) * dmask
        inner = _dot(scores.astype(BF16), vc)
        cross = _dot(qc, state.astype(BF16)) * q_decay
        out = inner + cross
        kd = (kc.astype(F32) * k_decay).astype(BF16)
        state = state * chunk_decay + _dot_tn(kd, vc)
        mu = jnp.mean(out, axis=-1, keepdims=True)
        cen = out - mu
        var = jnp.mean(cen * cen, axis=-1, keepdims=True)
        yn = cen * lax.rsqrt(var + EPS)
        y_ref[0, 0, sl, :] = (yn * jax.nn.silu(g_ref[0, 0, sl, :])).astype(BF16)


def _retention(q, k, v, g):
    b, h, s, d = q.shape
    lg = jnp.log1p(-jnp.exp2(-5.0 - jnp.arange(h, dtype=F32)))
    lg_rows = jnp.broadcast_to(lg[:, None, None], (h, 1, d))
    blk = pl.BlockSpec((1, 1, s, d), lambda bi, hi: (bi, hi, 0, 0))
    est = 2 * s * d * (2 + 2 + 2 + 4 + 2)
    return pl.pallas_call(
        functools.partial(_retention_kernel, n_chunks=s // CHUNK),
        grid=(b, h),
        in_specs=[pl.BlockSpec((1, 1, d), lambda bi, hi: (hi, 0, 0)), blk, blk, blk, blk],
        out_specs=blk,
        out_shape=jax.ShapeDtypeStruct((b, h, s, d), BF16),
        compiler_params=pltpu.CompilerParams(
            dimension_semantics=("parallel", "arbitrary"),
            vmem_limit_bytes=_vmem_limit(est + (8 << 20))),
        name="retention",
    )(lg_rows, q, k, v, g)


def _sb_kernel(q_ref, k_ref, v_ref, o_ref, *, t):
    qi = pl.program_id(2)
    scale = HEAD_DIM ** -0.5
    q = q_ref[0, 0]
    row = lax.broadcasted_iota(jnp.int32, (t, t), 0)
    col = lax.broadcasted_iota(jnp.int32, (t, t), 1)
    causal = col < row
    later = (row > col).astype(BF16)

    def tile(kb, carry, acc, diag):
        ks = pl.ds(pl.multiple_of(kb * t, t), t)
        k = k_ref[0, 0, ks, :]
        v = v_ref[0, 0, ks, :]
        z = _dot_nt(q, k) * scale
        sp = jnp.log1p(jnp.exp(-jnp.abs(z)))
        log_beta = jnp.minimum(z, 0.0) - sp
        log_1mb = -jnp.maximum(z, 0.0) - sp
        if diag:
            log_1mb = jnp.where(causal, log_1mb, 0.0)
        hi = log_1mb.astype(BF16)
        lo = (log_1mb - hi.astype(F32)).astype(BF16)
        after = carry + (_dot(hi, later) + _dot(lo, later))
        a = jnp.exp(log_beta + after)
        if diag:
            a = jnp.where(causal, a, 0.0)
        acc = acc + _dot(a.astype(BF16), v)
        carry = carry + jnp.sum(log_1mb, axis=-1, keepdims=True)
        return carry, acc

    carry = jnp.zeros((t, 1), F32)
    acc = jnp.zeros((t, HEAD_DIM), F32)
    carry, acc = tile(qi, carry, acc, True)

    def body(it, ca):
        return tile(qi - it, ca[0], ca[1], False)

    carry, acc = lax.fori_loop(1, qi + 1, body, (carry, acc))
    o_ref[0, 0] = acc.astype(o_ref.dtype)


def _stick_breaking(q, k, v, *, t):
    b, h, s, d = q.shape
    qblk = pl.BlockSpec((1, 1, t, d), lambda bi, hi, qi: (bi, hi, qi, 0))
    kvblk = pl.BlockSpec((1, 1, s, d), lambda bi, hi, qi: (bi, hi, 0, 0))
    est = 2 * 2 * s * d * 2 + 4 * t * d * 2 + 8 * t * t * 4
    return pl.pallas_call(
        functools.partial(_sb_kernel, t=t),
        grid=(b, h, s // t),
        in_specs=[qblk, kvblk, kvblk],
        out_specs=qblk,
        out_shape=jax.ShapeDtypeStruct((b, h, s, d), BF16),
        compiler_params=pltpu.CompilerParams(
            dimension_semantics=("parallel", "parallel", "arbitrary"),
            vmem_limit_bytes=_vmem_limit(est + (8 << 20))),
        name="stick_breaking",
    )(q, k, v)


def _outproj_kernel(y_ref, qm_ref, mk_ref, mv_ref, wo_ref, x_ref, gpost_ref, o_ref):
    scale = HEAD_DIM ** -0.5
    parts = [y_ref[0, hh] for hh in range(y_ref.shape[1])]
    for hh in range(qm_ref.shape[1]):
        s = _dot_nt(qm_ref[0, hh], mk_ref[0, hh]) * scale
        e = jnp.exp(s - jnp.max(s, axis=-1, keepdims=True))
        p = e / jnp.sum(e, axis=-1, keepdims=True)
        parts.append(_dot(p.astype(BF16), mv_ref[0, hh]).astype(BF16))
    lhs = jnp.concatenate(parts, axis=1)
    o_ref[...] = x_ref[...] + _rms(_dot(lhs, wo_ref[...]), gpost_ref[...])


def _outproj(y, qm, mk, mv, w_o, x2d, g_post, *, bm):
    b, hy, s, hd = y.shape
    hm, n_mem = mk.shape[1], mk.shape[2]
    m, d = x2d.shape
    spb = s // bm
    wo = w_o.astype(BF16)
    tok = lambda i: (i // spb, 0, i % spb, 0)
    bat = lambda i: (i // spb, 0, 0, 0)
    est = (2 * 2 * bm * d * 4 + 2 * wo.size * 2 + 2 * bm * (hy + hm) * hd * 2
           + 2 * 2 * hm * n_mem * hd * 2 + bm * d * (2 + 4) + 4 * bm * n_mem * 4)
    return pl.pallas_call(
        _outproj_kernel,
        grid=(m // bm,),
        in_specs=[
            pl.BlockSpec((1, hy, bm, hd), tok),
            pl.BlockSpec((1, hm, bm, hd), tok),
            pl.BlockSpec((1, hm, n_mem, hd), bat),
            pl.BlockSpec((1, hm, n_mem, hd), bat),
            pl.BlockSpec(wo.shape, lambda i: (0, 0)),
            pl.BlockSpec((bm, d), lambda i: (i, 0)),
            pl.BlockSpec((1, d), lambda i: (0, 0)),
        ],
        out_specs=pl.BlockSpec((bm, d), lambda i: (i, 0)),
        out_shape=jax.ShapeDtypeStruct((m, d), F32),
        compiler_params=pltpu.CompilerParams(
            dimension_semantics=("parallel",),
            vmem_limit_bytes=_vmem_limit(est)),
        name="outproj",
    )(y, qm, mk, mv, wo, x2d, g_post.reshape(1, d))


def kernel(x, mem, positions, ffn1_norm_pre, ffn1_norm_post, ffn1_w_gate, ffn1_w_up, ffn1_w_down,
           mix_norm_pre, mix_norm_post, mem_norm, w_mem_kv, w_o, ret_w_in, kv_norm, w_kv_shared,
           sb_w_in, ffn2_norm_pre, ffn2_norm_post, ffn2_w_gate, ffn2_w_up, ffn2_w_down):
    b, s, d = x.shape
    n_mem = mem.shape[1]
    depth = ffn1_w_gate.shape[0]
    n_a = ret_w_in.shape[0]
    assert d == MIX_W + MEM_W

    bm_ffn = min(512, s)
    bf = 512
    bm_proj = min(1024, s)
    bm_out = min(512, s)
    t_sb = min(256, s)

    x2 = x.reshape(b * s, d)
    mem2 = mem.reshape(b * n_mem, d)

    inv = ROPE_BASE ** (-jnp.arange(0, HEAD_DIM, 2, dtype=F32) / HEAD_DIM)
    inv2 = jnp.concatenate([inv, inv]).reshape(1, HEAD_DIM)
    pos = jnp.broadcast_to(positions.astype(F32).reshape(b * s, 1), (b * s, HEAD_DIM))

    mem_sections = ((MEM_W // PROJ_TILE, "plain", BF16),) * 2
    mix_tiles = MIX_W // PROJ_TILE
    mem_tiles = MEM_W // PROJ_TILE

    k_sh = v_sh = None
    for l in range(depth):
        if l == n_a:
            k_sh, v_sh = _proj(
                x2, kv_norm, w_kv_shared,
                ((mix_tiles, "plain", BF16), (mix_tiles, "plain", BF16)),
                batch=b, seq=s, bm=bm_proj)
        x2 = _ffn(x2, ffn1_norm_pre[l], ffn1_norm_post[l],
                  ffn1_w_gate[l], ffn1_w_up[l], ffn1_w_down[l], bm=bm_ffn, bf=bf)
        mk, mv = _proj(mem2, mem_norm[l], w_mem_kv[l], mem_sections,
                       batch=b, seq=n_mem, bm=n_mem)
        if l < n_a:
            q, k, v, g, qm = _proj(
                x2, mix_norm_pre[l], ret_w_in[l],
                ((mix_tiles, "rot", BF16), (mix_tiles, "rot_scaled", BF16),
                 (mix_tiles, "plain", BF16), (mix_tiles, "plain", F32),
                 (mem_tiles, "plain", BF16)),
                batch=b, seq=s, bm=bm_proj, pos=pos, inv2=inv2)
            y = _retention(q, k, v, g)
        else:
            q, qm = _proj(
                x2, mix_norm_pre[l], sb_w_in[l - n_a],
                ((mix_tiles, "plain", BF16), (mem_tiles, "plain", BF16)),
                batch=b, seq=s, bm=bm_proj)
            y = _stick_breaking(q, k_sh, v_sh, t=t_sb)
        x2 = _outproj(y, qm, mk, mv, w_o[l], x2, mix_norm_post[l], bm=bm_out)
        x2 = _ffn(x2, ffn2_norm_pre[l], ffn2_norm_post[l],
                  ffn2_w_gate[l], ffn2_w_up[l], ffn2_w_down[l], bm=bm_ffn, bf=bf)
    return x2.reshape(b, s, d)
```

```python
import functools

import jax
import jax.numpy as jnp
from jax import lax
from jax.experimental import pallas as pl
from jax.experimental.pallas import tpu as pltpu

HEAD_DIM = 128
N_MIX_HEADS = 12
N_MEM_HEADS = 4
MIX_W = N_MIX_HEADS * HEAD_DIM
MEM_W = N_MEM_HEADS * HEAD_DIM
CHUNK = 128
ROPE_BASE = 10000.0
EPS = 1e-6
MACARON_W = 0.5
LOG2_E = 1.4426950408889634

ROW_CHUNK = 256
HEADS_PER_TILE = 4
PROJ_TILE = HEADS_PER_TILE * HEAD_DIM
MIX_TILES = MIX_W // PROJ_TILE
MEM_TILES = MEM_W // PROJ_TILE
V7X_SCOPED_VMEM_CAP = 60000 * 1024

F32 = jnp.float32
BF16 = jnp.bfloat16


def _vmem_limit(estimate_bytes):
    return int(min(2 * estimate_bytes, V7X_SCOPED_VMEM_CAP))


def _rms(x, gain):
    ms = jnp.mean(x * x, axis=-1, keepdims=True)
    return x * lax.rsqrt(ms + EPS) * gain


def _dot(a, b):
    return jnp.dot(a, b, preferred_element_type=F32)


def _dot_nt(a, b):
    return lax.dot_general(a, b, (((1,), (1,)), ((), ())), preferred_element_type=F32)


def _dot_tn(a, b):
    return lax.dot_general(a, b, (((0,), (0,)), ((), ())), preferred_element_type=F32)


def _ffn_kernel(x_ref, gpre_ref, gpost_ref, wg_ref, wu_ref, wd_ref, o_ref, h_ref):
    j = pl.program_id(1)
    last = pl.num_programs(1) - 1
    bm = x_ref.shape[0]
    mc = min(ROW_CHUNK, bm)

    def step(first, final):
        for c in range(bm // mc):
            rows = pl.ds(c * mc, mc)
            if first:
                h = _rms(x_ref[rows, :], gpre_ref[...]).astype(BF16)
                h_ref[rows, :] = h
            else:
                h = h_ref[rows, :]
            a = (jax.nn.silu(_dot(h, wg_ref[...])) * _dot(h, wu_ref[...])).astype(BF16)
            y = _dot(a, wd_ref[...])
            if not first:
                y = o_ref[rows, :] + y
            if final:
                y = x_ref[rows, :] + MACARON_W * _rms(y, gpost_ref[...])
            o_ref[rows, :] = y

    pl.when(j == 0)(functools.partial(step, True, False))
    pl.when((j > 0) & (j < last))(functools.partial(step, False, False))
    pl.when(j == last)(functools.partial(step, False, True))


def _ffn(x2d, g_pre, g_post, w_gate, w_up, w_down, *, bm, bf):
    m, d = x2d.shape
    f = w_gate.shape[1]
    assert f // bf >= 2
    mc = min(ROW_CHUNK, bm)
    est = (2 * 2 * bm * d * 4
           + bm * d * 2
           + 2 * 3 * d * bf * 2
           + 2 * (2 * mc * bf * 4 + mc * d * 4))
    return pl.pallas_call(
        _ffn_kernel,
        grid=(m // bm, f // bf),
        in_specs=[
            pl.BlockSpec((bm, d), lambda i, j: (i, 0)),
            pl.BlockSpec((1, d), lambda i, j: (0, 0)),
            pl.BlockSpec((1, d), lambda i, j: (0, 0)),
            pl.BlockSpec((d, bf), lambda i, j: (0, j)),
            pl.BlockSpec((d, bf), lambda i, j: (0, j)),
            pl.BlockSpec((bf, d), lambda i, j: (j, 0)),
        ],
        out_specs=pl.BlockSpec((bm, d), lambda i, j: (i, 0)),
        out_shape=jax.ShapeDtypeStruct((m, d), F32),
        scratch_shapes=[pltpu.VMEM((bm, d), BF16)],
        compiler_params=pltpu.CompilerParams(
            dimension_semantics=("parallel", "arbitrary"),
            vmem_limit_bytes=_vmem_limit(est)),
        name="ffn",
    )(x2d, g_pre.reshape(1, d), g_post.reshape(1, d),
      w_gate.astype(BF16), w_up.astype(BF16), w_down.astype(BF16))


def _proj_kernel(*refs, n_rot, k_lo, k_hi, n_main, has_aux):
    rotary = n_rot > 0
    refs = list(refs)
    x_ref, gain_ref, w_ref = refs[:3]
    del refs[:3]
    if rotary:
        pos_ref, inv_ref = refs[:2]
        del refs[:2]
    out_ref = refs.pop(0)
    if has_aux:
        aux_ref = refs.pop(0)
    h_ref = refs.pop(0)
    if rotary:
        cos_ref, sin_ref = refs
    j = pl.program_id(1)
    bm = x_ref.shape[0]
    mc = min(ROW_CHUNK, bm)

    def step(first, aux):
        if rotary and not aux:
            is_rot = j < n_rot
            k_scale = jnp.where((j >= k_lo) & (j < k_hi), HEAD_DIM ** -0.5, 1.0).astype(F32)
        for c in range(bm // mc):
            rows = pl.ds(c * mc, mc)
            if first:
                h = _rms(x_ref[rows, :], gain_ref[...]).astype(BF16)
                h_ref[rows, :] = h
                if rotary:
                    ang = pos_ref[rows, :] * inv_ref[...]
                    lane = lax.broadcasted_iota(jnp.int32, ang.shape, 1)
                    sin = jnp.sin(ang)
                    cos_ref[rows, :] = jnp.cos(ang)
                    sin_ref[rows, :] = jnp.where(lane < HEAD_DIM // 2, -sin, sin)
            else:
                h = h_ref[rows, :]
            r = _dot(h, w_ref[...])
            for hh in range(HEADS_PER_TILE):
                v = r[:, hh * HEAD_DIM:(hh + 1) * HEAD_DIM]
                if aux:
                    aux_ref[0, hh, rows, :] = v
                    continue
                if rotary:
                    rot = (v * cos_ref[rows, :]
                           + pltpu.roll(v, HEAD_DIM // 2, 1) * sin_ref[rows, :]) * k_scale
                    v = jnp.where(is_rot, rot, v)
                out_ref[0, hh, rows, :] = v.astype(out_ref.dtype)

    pl.when(j == 0)(functools.partial(step, True, False))
    pl.when((j > 0) & (j < n_main))(functools.partial(step, False, False))
    if has_aux:
        pl.when(j >= n_main)(functools.partial(step, False, True))


def _proj(x2d, gain, w, *, batch, seq, bm, pos=None, inv2=None,
          n_rot=0, k_lo=0, k_hi=0, aux_lo=0, aux_hi=0):
    m, d = x2d.shape
    n_tiles = w.shape[1] // PROJ_TILE
    n_aux = aux_hi - aux_lo
    n_main = n_tiles - n_aux
    assert w.shape[1] == n_tiles * PROJ_TILE
    assert m == batch * seq and seq % bm == 0
    assert n_main >= 1 and (n_aux == 0 or aux_lo >= 1)
    rotary = n_rot > 0
    spb = seq // bm

    def w_map(i, j):
        if n_aux:
            j = jnp.where(j < aux_lo, j, jnp.where(j < n_main, j + n_aux, j - n_main + aux_lo))
        return (0, j)

    in_specs = [
        pl.BlockSpec((bm, d), lambda i, j: (i, 0)),
        pl.BlockSpec((1, d), lambda i, j: (0, 0)),
        pl.BlockSpec((d, PROJ_TILE), w_map),
    ]
    args = [x2d, gain.reshape(1, d), w.astype(BF16)]
    scratch = [pltpu.VMEM((bm, d), BF16)]
    if rotary:
        in_specs += [
            pl.BlockSpec((bm, HEAD_DIM), lambda i, j: (i, 0)),
            pl.BlockSpec((1, HEAD_DIM), lambda i, j: (0, 0)),
        ]
        args += [pos, inv2]
        scratch += [pltpu.VMEM((bm, HEAD_DIM), F32), pltpu.VMEM((bm, HEAD_DIM), F32)]

    blk = (1, HEADS_PER_TILE, bm, HEAD_DIM)
    out_specs = [pl.BlockSpec(
        blk, lambda i, j: (i // spb, jnp.minimum(j, n_main - 1), i % spb, 0))]
    out_shapes = [jax.ShapeDtypeStruct((batch, HEADS_PER_TILE * n_main, seq, HEAD_DIM), BF16)]
    if n_aux:
        out_specs.append(pl.BlockSpec(
            blk, lambda i, j: (i // spb, jnp.maximum(j - n_main, 0), i % spb, 0)))
        out_shapes.append(jax.ShapeDtypeStruct(
            (batch, HEADS_PER_TILE * n_aux, seq, HEAD_DIM), F32))

    mc = min(ROW_CHUNK, bm)
    est = (2 * bm * d * 4 + bm * d * 2 + 2 * d * PROJ_TILE * 2
           + 2 * bm * PROJ_TILE * (2 + (4 if n_aux else 0))
           + 4 * mc * PROJ_TILE * 4 + 4 * bm * HEAD_DIM * 4)
    outs = pl.pallas_call(
        functools.partial(_proj_kernel, n_rot=n_rot, k_lo=k_lo, k_hi=k_hi,
                          n_main=n_main, has_aux=n_aux > 0),
        grid=(m // bm, n_tiles),
        in_specs=in_specs,
        out_specs=out_specs,
        out_shape=out_shapes,
        scratch_shapes=scratch,
        compiler_params=pltpu.CompilerParams(
            dimension_semantics=("parallel", "arbitrary"),
            vmem_limit_bytes=_vmem_limit(est)),
        name="proj",
    )(*args)
    return outs if n_aux else outs[0]


def _retention_kernel(lg_ref, q_ref, k_ref, v_ref, g_ref, y_ref, *, n_chunks):
    c = CHUNK
    lg = lg_ref[0]
    row = lax.broadcasted_iota(jnp.int32, (c, c), 0).astype(F32)
    col = lax.broadcasted_iota(jnp.int32, (c, c), 1).astype(F32)
    diff = row - col
    dmask = jnp.where(diff >= 0, jnp.exp(lg * jnp.maximum(diff, 0.0)), 0.0)
    k_decay = jnp.exp(lg * (c - 1.0 - row))
    q_decay = jnp.exp(lg * (row + 1.0))
    chunk_decay = jnp.exp(lg * float(c))

    state = jnp.zeros((HEAD_DIM, HEAD_DIM), F32)
    for n in range(n_chunks):
        sl = pl.ds(n * c, c)
        qc = q_ref[0, 0, sl, :]
        kc = k_ref[0, 0, sl, :]
        vc = v_ref[0, 0, sl, :]
        scores = _dot_nt(qc, kc) * dmask
        inner = _dot(scores.astype(BF16), vc)
        cross = _dot(qc, state.astype(BF16)) * q_decay
        out = inner + cross
        kd = (kc.astype(F32) * k_decay).astype(BF16)
        state = state * chunk_decay + _dot_tn(kd, vc)
        mu = jnp.mean(out, axis=-1, keepdims=True)
        cen = out - mu
        var = jnp.mean(cen * cen, axis=-1, keepdims=True)
        yn = cen * lax.rsqrt(var + EPS)
        y_ref[0, 0, sl, :] = (yn * jax.nn.silu(g_ref[0, 0, sl, :])).astype(BF16)


def _retention(qkv, g):
    b, h, s, d = g.shape
    lg = jnp.log1p(-jnp.exp2(-5.0 - jnp.arange(h, dtype=F32)))
    lg_rows = jnp.broadcast_to(lg[:, None, None], (h, 1, d))
    blk = (1, 1, s, d)
    est = 2 * s * d * (2 + 2 + 2 + 4 + 2)
    return pl.pallas_call(
        functools.partial(_retention_kernel, n_chunks=s // CHUNK),
        grid=(b, h),
        in_specs=[
            pl.BlockSpec((1, 1, d), lambda bi, hi: (hi, 0, 0)),
            pl.BlockSpec(blk, lambda bi, hi: (bi, hi, 0, 0)),
            pl.BlockSpec(blk, lambda bi, hi: (bi, h + hi, 0, 0)),
            pl.BlockSpec(blk, lambda bi, hi: (bi, 2 * h + hi, 0, 0)),
            pl.BlockSpec(blk, lambda bi, hi: (bi, hi, 0, 0)),
        ],
        out_specs=pl.BlockSpec(blk, lambda bi, hi: (bi, hi, 0, 0)),
        out_shape=jax.ShapeDtypeStruct((b, h, s, d), BF16),
        compiler_params=pltpu.CompilerParams(
            dimension_semantics=("parallel", "arbitrary"),
            vmem_limit_bytes=_vmem_limit(est + (8 << 20))),
        name="retention",
    )(lg_rows, qkv, qkv, qkv, g)


def _sb_kernel(q_ref, k_ref, v_ref, o_ref, *, t):
    n_q = q_ref.shape[2] // t
    to_log2 = HEAD_DIM ** -0.5 * LOG2_E
    row = lax.broadcasted_iota(jnp.int32, (t, t), 0)
    col = lax.broadcasted_iota(jnp.int32, (t, t), 1)
    causal = col < row
    later = (row > col).astype(BF16)

    for n in range(n_q):
        rows = slice(n * t, (n + 1) * t)
        keys = (n + 1) * t
        z2 = _dot_nt(q_ref[0, 0, rows, :], k_ref[0, 0, :keys, :]) * to_log2
        sp = jnp.log2(1.0 + jnp.exp2(-jnp.abs(z2)))
        log_beta = jnp.minimum(z2, 0.0) - sp
        neg_log_1mb = jnp.maximum(z2, 0.0) + sp
        carry = jnp.zeros((t, 1), F32)
        a_tiles = [None] * (n + 1)
        for c in range(n, -1, -1):
            cols = slice(c * t, (c + 1) * t)
            l1 = neg_log_1mb[:, cols]
            if c == n:
                l1 = jnp.where(causal, l1, 0.0)
            hi = l1.astype(BF16)
            lo = (l1 - hi.astype(F32)).astype(BF16)
            after = carry + (_dot(hi, later) + _dot(lo, later))
            a = jnp.exp2(log_beta[:, cols] - after)
            if c == n:
                a = jnp.where(causal, a, 0.0)
            a_tiles[c] = a.astype(BF16)
            carry = carry + jnp.sum(l1, axis=-1, keepdims=True)
        a_all = a_tiles[0] if n == 0 else jnp.concatenate(a_tiles, axis=1)
        o_ref[0, 0, rows, :] = _dot(a_all, v_ref[0, 0, :keys, :]).astype(o_ref.dtype)


def _stick_breaking(qarr, kv, *, t):
    b, h2, s, d = kv.shape
    h = h2 // 2
    blk = (1, 1, s, d)
    est = 2 * 4 * s * d * 2 + 8 * t * s * 4
    return pl.pallas_call(
        functools.partial(_sb_kernel, t=t),
        grid=(b, h),
        in_specs=[
            pl.BlockSpec(blk, lambda bi, hi: (bi, hi, 0, 0)),
            pl.BlockSpec(blk, lambda bi, hi: (bi, hi, 0, 0)),
            pl.BlockSpec(blk, lambda bi, hi: (bi, h + hi, 0, 0)),
        ],
        out_specs=pl.BlockSpec(blk, lambda bi, hi: (bi, hi, 0, 0)),
        out_shape=jax.ShapeDtypeStruct((b, h, s, d), BF16),
        compiler_params=pltpu.CompilerParams(
            dimension_semantics=("parallel", "arbitrary"),
            vmem_limit_bytes=_vmem_limit(est + (8 << 20))),
        name="stick_breaking",
    )(qarr, kv, kv)


def _outproj_kernel(y_ref, qm_ref, mk_ref, mv_ref, wo_ref, x_ref, gpost_ref, o_ref):
    scale = HEAD_DIM ** -0.5
    parts = [y_ref[0, hh] for hh in range(y_ref.shape[1])]
    for hh in range(qm_ref.shape[1]):
        s = _dot_nt(qm_ref[0, hh], mk_ref[0, hh]) * scale
        e = jnp.exp(s - jnp.max(s, axis=-1, keepdims=True))
        p = e / jnp.sum(e, axis=-1, keepdims=True)
        parts.append(_dot(p.astype(BF16), mv_ref[0, hh]).astype(BF16))
    lhs = jnp.concatenate(parts, axis=1)
    o_ref[...] = x_ref[...] + _rms(_dot(lhs, wo_ref[...]), gpost_ref[...])


def _outproj(y, qarr, qm_blk, memkv, w_o, x2d, g_post, *, bm):
    b, hy, s, hd = y.shape
    hm = HEADS_PER_TILE
    n_mem = memkv.shape[2]
    m, d = x2d.shape
    spb = s // bm
    wo = w_o.astype(BF16)
    est = (2 * 2 * bm * d * 4 + 2 * wo.size * 2 + 2 * bm * (hy + hm) * hd * 2
           + 2 * 2 * hm * n_mem * hd * 2 + bm * d * (2 + 4) + 4 * bm * n_mem * 4)
    return pl.pallas_call(
        _outproj_kernel,
        grid=(m // bm,),
        in_specs=[
            pl.BlockSpec((1, hy, bm, hd), lambda i: (i // spb, 0, i % spb, 0)),
            pl.BlockSpec((1, hm, bm, hd), lambda i: (i // spb, qm_blk, i % spb, 0)),
            pl.BlockSpec((1, hm, n_mem, hd), lambda i: (i // spb, 0, 0, 0)),
            pl.BlockSpec((1, hm, n_mem, hd), lambda i: (i // spb, 1, 0, 0)),
            pl.BlockSpec(wo.shape, lambda i: (0, 0)),
            pl.BlockSpec((bm, d), lambda i: (i, 0)),
            pl.BlockSpec((1, d), lambda i: (0, 0)),
        ],
        out_specs=pl.BlockSpec((bm, d), lambda i: (i, 0)),
        out_shape=jax.ShapeDtypeStruct((m, d), F32),
        compiler_params=pltpu.CompilerParams(
            dimension_semantics=("parallel",),
            vmem_limit_bytes=_vmem_limit(est)),
        name="outproj",
    )(y, qarr, memkv, memkv, wo, x2d, g_post.reshape(1, d))


def kernel(x, mem, positions, ffn1_norm_pre, ffn1_norm_post, ffn1_w_gate, ffn1_w_up, ffn1_w_down,
           mix_norm_pre, mix_norm_post, mem_norm, w_mem_kv, w_o, ret_w_in, kv_norm, w_kv_shared,
           sb_w_in, ffn2_norm_pre, ffn2_norm_post, ffn2_w_gate, ffn2_w_up, ffn2_w_down):
    b, s, d = x.shape
    n_mem = mem.shape[1]
    depth = ffn1_w_gate.shape[0]
    n_a = ret_w_in.shape[0]
    assert d == MIX_W + MEM_W

    bm_ffn = min(512, s)
    bf = 512
    bm_proj = min(1024, s)
    bm_out = min(512, s)
    t_sb = min(256, s)

    x2 = x.reshape(b * s, d)
    mem2 = mem.reshape(b * n_mem, d)

    inv = ROPE_BASE ** (-jnp.arange(0, HEAD_DIM, 2, dtype=F32) / HEAD_DIM)
    inv2 = jnp.concatenate([inv, inv]).reshape(1, HEAD_DIM)
    pos = jnp.broadcast_to(positions.astype(F32).reshape(b * s, 1), (b * s, HEAD_DIM))

    kv_sh = None
    for l in range(depth):
        if l == n_a:
            kv_sh = _proj(x2, kv_norm, w_kv_shared, batch=b, seq=s, bm=bm_proj)
        x2 = _ffn(x2, ffn1_norm_pre[l], ffn1_norm_post[l],
                  ffn1_w_gate[l], ffn1_w_up[l], ffn1_w_down[l], bm=bm_ffn, bf=bf)
        memkv = _proj(mem2, mem_norm[l], w_mem_kv[l], batch=b, seq=n_mem, bm=n_mem)
        if l < n_a:
            qkvm, g = _proj(
                x2, mix_norm_pre[l], ret_w_in[l], batch=b, seq=s, bm=bm_proj, pos=pos, inv2=inv2,
                n_rot=2 * MIX_TILES, k_lo=MIX_TILES, k_hi=2 * MIX_TILES,
                aux_lo=3 * MIX_TILES, aux_hi=4 * MIX_TILES)
            y = _retention(qkvm, g)
            qarr, qm_blk = qkvm, 3 * MIX_TILES
        else:
            qarr = _proj(x2, mix_norm_pre[l], sb_w_in[l - n_a], batch=b, seq=s, bm=bm_proj)
            y = _stick_breaking(qarr, kv_sh, t=t_sb)
            qm_blk = MIX_TILES
        x2 = _outproj(y, qarr, qm_blk, memkv, w_o[l], x2, mix_norm_post[l], bm=bm_out)
        x2 = _ffn(x2, ffn2_norm_pre[l], ffn2_norm_post[l],
                  ffn2_w_gate[l], ffn2_w_up[l], ffn2_w_down[l], bm=bm_ffn, bf=bf)
    return x2.reshape(b, s, d)
```
